```python
import math
import jax, jax.numpy as jnp
from jax import lax
import numpy as np

D_MODEL = 2048
BATCH = 1
SEQ = 16384
DEPTH = 2
DEC_BATCH = 16
DEC_SEQ = 16
PAST_LEN = 2048

CHUNK = 64
N_META = 16
N_A = DEPTH // 2
N_B = DEPTH - N_A
D_FF = 5632
D_RNN = D_MODEL
N_RG_BLOCKS = 16
RG_BLOCK = D_RNN // N_RG_BLOCKS
CONV_W = 4
LRU_C = 8.0
N_HEADS = 16
HEAD_DIM = D_MODEL // (2 * N_HEADS)
V_HEAD_DIM = 2 * HEAD_DIM
QK_WIDTH = N_HEADS * 2 * HEAD_DIM
V_WIDTH = N_HEADS * V_HEAD_DIM
Q_BLOCK = 128
EPS = 1e-6

kernel_name = 'hybrid_rglru_diffattn_yoco_stream_step'


def _rms(x, g):
    x32 = x.astype(jnp.float32)
    y = x32 * lax.rsqrt(jnp.mean(x32 * x32, axis=-1, keepdims=True) + EPS)
    return (y * g.astype(jnp.float32)).astype(x.dtype)


def _swiglu(u, w_gate, w_up, w_down):
    return (jax.nn.silu(u @ w_gate) * (u @ w_up)) @ w_down


def _alibi_slopes():
    return 2.0 ** (-8.0 * jnp.arange(1, N_HEADS + 1, dtype=jnp.float32) / N_HEADS)


def _lambda_init(layer_idx):
    return 0.8 - 0.6 * math.exp(-0.3 * layer_idx)


def _lin_combine(e1, e2):
    a1, b1 = e1
    a2, b2 = e2
    return a1 * a2, a2 * b1 + b2


def _rglru_mixer(u, conv_state, h_state, w_in, conv_w, conv_b, gate_w, gate_b, lam, w_out):
    B, T, _ = u.shape
    proj = u @ w_in
    gate_br, xb = proj[..., :D_RNN], proj[..., D_RNN:]
    xp = jnp.concatenate([conv_state.astype(xb.dtype), xb], axis=1)
    new_conv = xp[:, T:]
    xc = conv_b
    for k in range(CONV_W):
        xc = xc + xp[:, k:k + T] * conv_w[k]
    xg = xc.reshape(B, T, N_RG_BLOCKS, RG_BLOCK)
    g = jnp.einsum('btnc,kncd->kbtnd', xg, gate_w).reshape(2, B, T, D_RNN) + gate_b[:, None, None, :]
    r = jax.nn.sigmoid(g[0].astype(jnp.float32))
    i = jax.nn.sigmoid(g[1].astype(jnp.float32))
    log_a = LRU_C * r * jax.nn.log_sigmoid(lam.astype(jnp.float32))
    a = jnp.exp(log_a)
    b = jnp.sqrt(-jnp.expm1(2.0 * log_a)) * (i * xc.astype(jnp.float32))
    b = b.at[:, 0].add(a[:, 0] * h_state.astype(jnp.float32))
    _, h = lax.associative_scan(_lin_combine, (a, b), axis=1)
    y = (jax.nn.gelu(gate_br) * h.astype(u.dtype)) @ w_out
    return y, new_conv, h[:, -1].astype(h_state.dtype)


def _diff_attn_block(q, q_pos, q_chunk, k, v, k_pos, k_chunk, lam, sub_g, lam_init):
    s = jnp.einsum('bqhcd,bkhcd->bchqk', q, k).astype(jnp.float32) * (HEAD_DIM ** -0.5)
    dist = jnp.abs(q_pos[:, None] - k_pos[None, :]).astype(jnp.float32)
    bias = -_alibi_slopes()[:, None, None] * dist[None]
    allowed = k_chunk[None, :] <= q_chunk[:, None]
    s = jnp.where(allowed, s + bias, -jnp.inf)
    p = jax.nn.softmax(s, axis=-1)
    w = p[:, 0] - lam * p[:, 1]
    o = jnp.einsum('bhqk,bkhe->bqhe', w.astype(v.dtype), v)
    return _rms(o, sub_g) * (1.0 - lam_init)


def _diff_attn_mixer(u, k_all, v_all, k_pos, k_chunk, q_pos, q_chunk, w_q, q_g, lam_vecs, sub_g, w_o, lam_init):
    B, T, _ = u.shape
    q = _rms((u @ w_q).reshape(B, T, N_HEADS, 2, HEAD_DIM), q_g)
    lv = lam_vecs.astype(jnp.float32)
    lam = jnp.exp(jnp.sum(lv[0] * lv[1])) - jnp.exp(jnp.sum(lv[2] * lv[3])) + lam_init
    if T <= Q_BLOCK:
        o = _diff_attn_block(q, q_pos, q_chunk, k_all, v_all, k_pos, k_chunk, lam, sub_g, lam_init)
    else:
        nb = T // Q_BLOCK
        qb = jnp.moveaxis(q.reshape(B, nb, Q_BLOCK, N_HEADS, 2, HEAD_DIM), 1, 0)
        ob = lax.map(lambda a: _diff_attn_block(a[0], a[1], a[2], k_all, v_all, k_pos, k_chunk, lam, sub_g, lam_init),
                     (qb, q_pos.reshape(nb, Q_BLOCK), q_chunk.reshape(nb, Q_BLOCK)))
        o = jnp.moveaxis(ob, 0, 1)
    return o.reshape(B, T, V_WIDTH) @ w_o


def _shared_kv(h, kv_g, w_kv, k_g):
    B, T, _ = h.shape
    kv = _rms(h, kv_g) @ w_kv
    k = _rms(kv[..., :QK_WIDTH].reshape(B, T, N_HEADS, 2, HEAD_DIM), k_g)
    v = kv[..., QK_WIDTH:].reshape(B, T, N_HEADS, V_HEAD_DIM)
    return k, v


def setup_inputs(seed: int = 0) -> dict:
    key = jax.random.key(seed)
    ks = jax.random.split(key, 32)
    f32 = jnp.float32

    def nrm(k, shape, scale=1.0):
        return jax.random.normal(k, shape, f32) * scale

    def gain(k, shape):
        return 1.0 + 0.01 * jax.random.normal(k, shape, f32)

    a0 = jax.random.uniform(ks[17], (N_A, D_RNN), f32, 0.9, 0.999)
    return {
        'x_prompt': nrm(ks[0], (BATCH, SEQ, D_MODEL)),
        'x_sample': nrm(ks[1], (DEC_BATCH, DEC_SEQ, D_MODEL)),
        'cache_k': nrm(ks[2], (DEC_BATCH, PAST_LEN, N_HEADS, 2, HEAD_DIM)),
        'cache_v': nrm(ks[3], (DEC_BATCH, PAST_LEN, N_HEADS, V_HEAD_DIM)),
        'state_conv': nrm(ks[4], (N_A, DEC_BATCH, CONV_W - 1, D_RNN)),
        'state_h': nrm(ks[5], (N_A, DEC_BATCH, D_RNN), 0.5),
        'meta_tokens': nrm(ks[6], (N_META, D_MODEL)),
        'ffn_norm': gain(ks[7], (DEPTH, 2, D_MODEL)),
        'ffn_w_gate': nrm(ks[8], (DEPTH, 2, D_MODEL, D_FF), D_MODEL ** -0.5),
        'ffn_w_up': nrm(ks[9], (DEPTH, 2, D_MODEL, D_FF), D_MODEL ** -0.5),
        'ffn_w_down': nrm(ks[10], (DEPTH, 2, D_FF, D_MODEL), D_FF ** -0.5),
        'rg_norm': gain(ks[11], (N_A, D_MODEL)),
        'rg_w_in': nrm(ks[12], (N_A, D_MODEL, 2 * D_RNN), D_MODEL ** -0.5),
        'rg_conv_w': nrm(ks[13], (N_A, CONV_W, D_RNN), CONV_W ** -0.5),
        'rg_conv_b': nrm(ks[14], (N_A, D_RNN), 0.01),
        'rg_gate_w': nrm(ks[15], (N_A, 2, N_RG_BLOCKS, RG_BLOCK, RG_BLOCK), RG_BLOCK ** -0.5),
        'rg_gate_b': nrm(ks[16], (N_A, 2, D_RNN), 0.01),
        'rg_lambda': jnp.log(a0) - jnp.log1p(-a0),
        'rg_w_out': nrm(ks[18], (N_A, D_RNN, D_MODEL), D_RNN ** -0.5),
        'kv_norm': gain(ks[19], (D_MODEL,)),
        'w_kv': nrm(ks[20], (D_MODEL, QK_WIDTH + V_WIDTH), D_MODEL ** -0.5),
        'k_norm': gain(ks[21], (HEAD_DIM,)),
        'attn_norm': gain(ks[22], (N_B, D_MODEL)),
        'w_q': nrm(ks[23], (N_B, D_MODEL, QK_WIDTH), D_MODEL ** -0.5),
        'q_norm': gain(ks[24], (N_B, HEAD_DIM)),
        'diff_lambda': nrm(ks[25], (N_B, 4, HEAD_DIM), 0.1),
        'sub_norm': gain(ks[26], (N_B, V_HEAD_DIM)),
        'w_o': nrm(ks[27], (N_B, V_WIDTH, D_MODEL), V_WIDTH ** -0.5),
    }


def reference(x_prompt, x_sample, cache_k, cache_v, state_conv, state_h, meta_tokens,
              ffn_norm, ffn_w_gate, ffn_w_up, ffn_w_down,
              rg_norm, rg_w_in, rg_conv_w, rg_conv_b, rg_gate_w, rg_gate_b, rg_lambda, rg_w_out,
              kv_norm, w_kv, k_norm,
              attn_norm, w_q, q_norm, diff_lambda, sub_norm, w_o):
    def ffn_half(x, l, s):
        u = _rms(x, ffn_norm[l, s])
        return x + 0.5 * _swiglu(u, ffn_w_gate[l, s], ffn_w_up[l, s], ffn_w_down[l, s])

    def rg_layer(x, conv_st, h_st, l):
        x = ffn_half(x, l, 0)
        y, conv_new, h_new = _rglru_mixer(_rms(x, rg_norm[l]), conv_st, h_st, rg_w_in[l], rg_conv_w[l],
                                          rg_conv_b[l], rg_gate_w[l], rg_gate_b[l], rg_lambda[l], rg_w_out[l])
        return ffn_half(x + y, l, 1), conv_new, h_new

    def diff_layer(x, j, k_all, v_all, k_pos, k_chunk, q_pos, q_chunk):
        l = N_A + j
        x = ffn_half(x, l, 0)
        x = x + _diff_attn_mixer(_rms(x, attn_norm[j]), k_all, v_all, k_pos, k_chunk, q_pos, q_chunk,
                                 w_q[j], q_norm[j], diff_lambda[j], sub_norm[j], w_o[j], _lambda_init(l))
        return ffn_half(x, l, 1)

    B, T_p, _ = x_prompt.shape
    B_s, T_s, _ = x_sample.shape
    past = cache_k.shape[1]
    i32 = jnp.int32
    meta_chunk = jnp.full((N_META,), -1, i32)
    kp_pos = jnp.arange(N_META + T_p, dtype=i32)
    kp_chunk = jnp.concatenate([meta_chunk, jnp.arange(T_p, dtype=i32) // CHUNK])
    qp_pos = N_META + jnp.arange(T_p, dtype=i32)
    qp_chunk = jnp.arange(T_p, dtype=i32) // CHUNK
    ks_pos = jnp.arange(N_META + past + T_s, dtype=i32)
    ks_chunk = jnp.concatenate([meta_chunk, jnp.arange(past + T_s, dtype=i32) // CHUNK])
    qs_pos = N_META + past + jnp.arange(T_s, dtype=i32)
    qs_chunk = (past + jnp.arange(T_s, dtype=i32)) // CHUNK

    x_m = meta_tokens[None]
    conv_zero = jnp.zeros((1, CONV_W - 1, D_RNN), meta_tokens.dtype)
    h_zero = jnp.zeros((1, D_RNN), meta_tokens.dtype)
    x_p, x_s = x_prompt, x_sample
    conv_p, h_p, conv_s, h_s = [], [], [], []
    for l in range(DEPTH):
        if l < N_A:
            x_m, conv_m, h_m = rg_layer(x_m, conv_zero, h_zero, l)
            x_p, c, h = rg_layer(x_p, jnp.broadcast_to(conv_m, (B, CONV_W - 1, D_RNN)),
                                 jnp.broadcast_to(h_m, (B, D_RNN)), l)
            conv_p.append(c)
            h_p.append(h)
            x_s, c, h = rg_layer(x_s, state_conv[l], state_h[l], l)
            conv_s.append(c)
            h_s.append(h)
        else:
            if l == N_A:
                k_m, v_m = _shared_kv(x_m, kv_norm, w_kv, k_norm)
                k_p, v_p = _shared_kv(x_p, kv_norm, w_kv, k_norm)
                k_s, v_s = _shared_kv(x_s, kv_norm, w_kv, k_norm)
                k_p_all = jnp.concatenate([jnp.broadcast_to(k_m, (B,) + k_m.shape[1:]), k_p], axis=1)
                v_p_all = jnp.concatenate([jnp.broadcast_to(v_m, (B,) + v_m.shape[1:]), v_p], axis=1)
                k_s_all = jnp.concatenate([jnp.broadcast_to(k_m, (B_s,) + k_m.shape[1:]),
                                           cache_k.astype(k_s.dtype), k_s], axis=1)
                v_s_all = jnp.concatenate([jnp.broadcast_to(v_m, (B_s,) + v_m.shape[1:]),
                                           cache_v.astype(v_s.dtype), v_s], axis=1)
            x_p = diff_layer(x_p, l - N_A, k_p_all, v_p_all, kp_pos, kp_chunk, qp_pos, qp_chunk)
            x_s = diff_layer(x_s, l - N_A, k_s_all, v_s_all, ks_pos, ks_chunk, qs_pos, qs_chunk)

    new_conv_p = jnp.stack(conv_p)
    new_h_p = jnp.stack(h_p)
    new_conv_s = jnp.stack(conv_s)
    new_h_s = jnp.stack(h_s)
    return (x_p, x_s, k_p_all, v_p_all, new_conv_p, new_h_p, k_s, v_s, new_conv_s, new_h_s)
```

```python
import functools
import math

import jax
import jax.numpy as jnp
from jax import lax
from jax.experimental import pallas as pl
from jax.experimental.pallas import tpu as pltpu

F32 = jnp.float32
BF16 = jnp.bfloat16

EPS = 1e-6
CHUNK = 64
N_META = 16
CONV_W = 4
LRU_C = 8.0
N_HEADS = 16
HEAD_DIM = 64
V_HEAD_DIM = 2 * HEAD_DIM
RG_BLOCK = 128
LANES = 128
VMEM_LIMIT = 56 * 1024 * 1024


def _params(*sem):
    return pltpu.CompilerParams(dimension_semantics=sem, vmem_limit_bytes=VMEM_LIMIT)


def _rms_rows(x, g):
    ms = jnp.mean(x * x, axis=-1, keepdims=True)
    return x * lax.rsqrt(ms + EPS) * g


def _dot(a, b):
    return jnp.dot(a, b, preferred_element_type=F32)


def _dot_nt(a, b):
    return lax.dot_general(a, b, (((1,), (1,)), ((), ())), preferred_element_type=F32)


def _ffn_kernel(x_ref, g_ref, wg_ref, wu_ref, wd_ref, o_ref, u_ref):
    j = pl.program_id(1)

    @pl.when(j == 0)
    def _():
        x = x_ref[...]
        u_ref[...] = _rms_rows(x, g_ref[...]).astype(BF16)
        o_ref[...] = x

    u = u_ref[...]
    a = _dot(u, wg_ref[...])
    b = _dot(u, wu_ref[...])
    h = (a * jax.nn.sigmoid(a)) * b
    o_ref[...] += 0.5 * _dot(h.astype(BF16), wd_ref[...])


def _ffn_half(x, g, wg, wu, wd, *, tm, tf):
    m, d = x.shape
    dff = wg.shape[1]
    return pl.pallas_call(
        _ffn_kernel,
        grid=(m // tm, dff // tf),
        in_specs=[
            pl.BlockSpec((tm, d), lambda i, j: (i, 0)),
            pl.BlockSpec((1, d), lambda i, j: (0, 0)),
            pl.BlockSpec((d, tf), lambda i, j: (0, j)),
            pl.BlockSpec((d, tf), lambda i, j: (0, j)),
            pl.BlockSpec((tf, d), lambda i, j: (j, 0)),
        ],
        out_specs=pl.BlockSpec((tm, d), lambda i, j: (i, 0)),
        out_shape=jax.ShapeDtypeStruct((m, d), F32),
        scratch_shapes=[pltpu.VMEM((tm, d), BF16)],
        compiler_params=_params("parallel", "arbitrary"),
        name="ffn_half",
    )(x, g.reshape(1, d), wg, wu, wd)


def _norm_matmul_kernel(x_ref, g_ref, w_ref, o_ref, u_ref):
    @pl.when(pl.program_id(1) == 0)
    def _():
        u_ref[...] = _rms_rows(x_ref[...], g_ref[...]).astype(BF16)

    o_ref[...] = _dot(u_ref[...], w_ref[...])


def _norm_matmul(x, g, w, *, tm, tn):
    m, d = x.shape
    n = w.shape[1]
    return pl.pallas_call(
        _norm_matmul_kernel,
        grid=(m // tm, n // tn),
        in_specs=[
            pl.BlockSpec((tm, d), lambda i, j: (i, 0)),
            pl.BlockSpec((1, d), lambda i, j: (0, 0)),
            pl.BlockSpec((d, tn), lambda i, j: (0, j)),
        ],
        out_specs=pl.BlockSpec((tm, tn), lambda i, j: (i, j)),
        out_shape=jax.ShapeDtypeStruct((m, n), F32),
        scratch_shapes=[pltpu.VMEM((tm, d), BF16)],
        compiler_params=_params("parallel", "arbitrary"),
        name="norm_matmul",
    )(x, g.reshape(1, d), w)


def _head_rms(y, g_row):
    lane = lax.broadcasted_iota(jnp.int32, y.shape, 1)
    lo = lane < HEAD_DIM
    sq = y * y
    s_lo = jnp.sum(jnp.where(lo, sq, 0.0), axis=-1, keepdims=True)
    s_hi = jnp.sum(jnp.where(lo, 0.0, sq), axis=-1, keepdims=True)
    inv = jnp.where(lo, lax.rsqrt(s_lo * (1.0 / HEAD_DIM) + EPS), lax.rsqrt(s_hi * (1.0 / HEAD_DIM) + EPS))
    return y * inv * g_row


def _kv_kernel(x_ref, g_ref, wk_ref, wv_ref, kg_ref, k_ref, v_ref, khm_ref, vhm_ref, u_ref, *, heads):
    @pl.when(pl.program_id(1) == 0)
    def _():
        u_ref[...] = _rms_rows(x_ref[...], g_ref[...]).astype(BF16)

    u = u_ref[...]
    k = _dot(u, wk_ref[...])
    v = _dot(u, wv_ref[...])
    v_ref[...] = v
    for hh in range(heads):
        sl = slice(hh * LANES, (hh + 1) * LANES)
        kh = _head_rms(k[:, sl], kg_ref[...])
        k_ref[:, sl] = kh
        khm_ref[hh] = kh.astype(BF16)
        vhm_ref[hh] = v[:, sl].astype(BF16)


def _kv_proj(x, g, w_kv, k_g, *, tm, heads=4):
    m, d = x.shape
    qk = w_kv.shape[1] // 2
    tn = heads * LANES
    nj = qk // tn
    kg_row = jnp.tile(k_g, 2).reshape(1, LANES)
    return pl.pallas_call(
        functools.partial(_kv_kernel, heads=heads),
        grid=(m // tm, nj),
        in_specs=[
            pl.BlockSpec((tm, d), lambda i, j: (i, 0)),
            pl.BlockSpec((1, d), lambda i, j: (0, 0)),
            pl.BlockSpec((d, tn), lambda i, j: (0, j)),
            pl.BlockSpec((d, tn), lambda i, j: (0, j + nj)),
            pl.BlockSpec((1, LANES), lambda i, j: (0, 0)),
        ],
        out_specs=[
            pl.BlockSpec((tm, tn), lambda i, j: (i, j)),
            pl.BlockSpec((tm, tn), lambda i, j: (i, j)),
            pl.BlockSpec((heads, tm, LANES), lambda i, j: (j, i, 0)),
            pl.BlockSpec((heads, tm, LANES), lambda i, j: (j, i, 0)),
        ],
        out_shape=[
            jax.ShapeDtypeStruct((m, qk), F32),
            jax.ShapeDtypeStruct((m, qk), F32),
            jax.ShapeDtypeStruct((N_HEADS, m, LANES), BF16),
            jax.ShapeDtypeStruct((N_HEADS, m, LANES), BF16),
        ],
        scratch_shapes=[pltpu.VMEM((tm, d), BF16)],
        compiler_params=_params("parallel", "arbitrary"),
        name="kv_proj",
    )(x, g.reshape(1, d), w_kv, w_kv, kg_row)


def _q_kernel(x_ref, g_ref, w_ref, qg_ref, qhm_ref, u_ref, *, heads):
    @pl.when(pl.program_id(1) == 0)
    def _():
        u_ref[...] = _rms_rows(x_ref[...], g_ref[...]).astype(BF16)

    q = _dot(u_ref[...], w_ref[...])
    for hh in range(heads):
        sl = slice(hh * LANES, (hh + 1) * LANES)
        qh = _head_rms(q[:, sl], qg_ref[...]) * (HEAD_DIM ** -0.5)
        qhm_ref[hh] = qh.astype(BF16)


def _q_proj(x, g, w_q, q_g, *, tm, heads=4):
    m, d = x.shape
    n = w_q.shape[1]
    tn = heads * LANES
    qg_row = jnp.tile(q_g, 2).reshape(1, LANES)
    return pl.pallas_call(
        functools.partial(_q_kernel, heads=heads),
        grid=(m // tm, n // tn),
        in_specs=[
            pl.BlockSpec((tm, d), lambda i, j: (i, 0)),
            pl.BlockSpec((1, d), lambda i, j: (0, 0)),
            pl.BlockSpec((d, tn), lambda i, j: (0, j)),
            pl.BlockSpec((1, LANES), lambda i, j: (0, 0)),
        ],
        out_specs=pl.BlockSpec((heads, tm, LANES), lambda i, j: (j, i, 0)),
        out_shape=jax.ShapeDtypeStruct((N_HEADS, m, LANES), BF16),
        scratch_shapes=[pltpu.VMEM((tm, d), BF16)],
        compiler_params=_params("parallel", "arbitrary"),
        name="q_proj",
    )(x, g.reshape(1, d), w_q, qg_row)


def _matmul_res_kernel(z_ref, w_ref, x_ref, o_ref):
    o_ref[...] = x_ref[...] + _dot(z_ref[...], w_ref[...])


def _matmul_res(z, w, x, *, tm, tn):
    m, kdim = z.shape
    n = w.shape[1]
    return pl.pallas_call(
        _matmul_res_kernel,
        grid=(m // tm, n // tn),
        in_specs=[
            pl.BlockSpec((tm, kdim), lambda i, j: (i, 0)),
            pl.BlockSpec((kdim, tn), lambda i, j: (0, j)),
            pl.BlockSpec((tm, tn), lambda i, j: (i, j)),
        ],
        out_specs=pl.BlockSpec((tm, tn), lambda i, j: (i, j)),
        out_shape=jax.ShapeDtypeStruct((m, n), F32),
        compiler_params=_params("parallel", "arbitrary"),
        name="matmul_res",
    )(z, w, x)


def _expm1_nonpos(x):
    u = jnp.exp(x)
    near = x > -0.5
    safe_log = jnp.log(jnp.where(near, u, 0.5))
    stable = jnp.where(u == 1.0, x, (u - 1.0) * x / safe_log)
    return jnp.where(near, stable, u - 1.0)


def _rg_kernel(gate_ref, xb_ref, cs_ref, hs_ref, cw_ref, cb_ref, gw_ref, gb_ref, lam_ref,
               z_ref, nc_ref, nh_ref, xs_ref, hc_ref, *, tt, tc):
    t = pl.program_id(2)
    pad = 8

    @pl.when(t == 0)
    def _():
        xs_ref[pad - (CONV_W - 1):pad, :] = cs_ref[...]
        hc_ref[...] = hs_ref[...]

    xs_ref[pad:pad + tt, :] = xb_ref[...]
    cw = cw_ref[...]
    xc = cb_ref[...] + cw[CONV_W - 1:CONV_W, :] * xb_ref[...]
    for k in range(CONV_W - 1):
        off = pad - (CONV_W - 1) + k
        xc = xc + cw[k:k + 1, :] * xs_ref[off:off + tt, :]

    gb = gb_ref[...]
    g_r, g_i = [], []
    for n in range(tc // RG_BLOCK):
        sl = slice(n * RG_BLOCK, (n + 1) * RG_BLOCK)
        xcb = xc[:, sl].astype(BF16)
        g_r.append(_dot(xcb, gw_ref[0, n]))
        g_i.append(_dot(xcb, gw_ref[1, n]))
    g_r = jnp.concatenate(g_r, axis=1) + gb[0:1, :]
    g_i = jnp.concatenate(g_i, axis=1) + gb[1:2, :]
    r = jax.nn.sigmoid(g_r)
    ig = jax.nn.sigmoid(g_i)
    lam = lam_ref[...]
    log_sig = jnp.minimum(lam, 0.0) - jnp.log1p(jnp.exp(-jnp.abs(lam)))
    log_a = LRU_C * r * log_sig
    a = jnp.exp(log_a)
    b = jnp.sqrt(-_expm1_nonpos(2.0 * log_a)) * (ig * xc)

    row = lax.broadcasted_iota(jnp.int32, (tt, tc), 0)
    d = 1
    while d < tt:
        keep = row >= d
        a_sh = pltpu.roll(a, d, axis=0)
        b_sh = pltpu.roll(b, d, axis=0)
        b = jnp.where(keep, a * b_sh + b, b)
        a = jnp.where(keep, a * a_sh, a)
        d *= 2
    h = a * hc_ref[...] + b

    z_ref[...] = (jax.nn.gelu(gate_ref[...]) * h).astype(z_ref.dtype)
    hc_ref[...] = h[tt - 1:tt, :]
    xs_ref[0:pad, :] = xs_ref[tt:tt + pad, :]

    @pl.when(t == pl.num_programs(2) - 1)
    def _():
        nc_ref[...] = xs_ref[tt + pad - (CONV_W - 1):tt + pad, :]
        nh_ref[...] = h[tt - 1:tt, :]


def _rg_core(proj, conv_state, h_state, conv_w, conv_b, gate_w, gate_b, lam, *, tt, tc):
    bsz, t_len, two_d = proj.shape
    d = two_d // 2
    nc = d // tc
    nb = tc // RG_BLOCK
    kern = functools.partial(_rg_kernel, tt=tt, tc=tc)
    return pl.pallas_call(
        kern,
        grid=(bsz, nc, t_len // tt),
        in_specs=[
            pl.BlockSpec((None, tt, tc), lambda b, c, t: (b, t, c)),
            pl.BlockSpec((None, tt, tc), lambda b, c, t: (b, t, c + nc)),
            pl.BlockSpec((None, CONV_W - 1, tc), lambda b, c, t: (b, 0, c)),
            pl.BlockSpec((None, 1, tc), lambda b, c, t: (b, 0, c)),
            pl.BlockSpec((CONV_W, tc), lambda b, c, t: (0, c)),
            pl.BlockSpec((1, tc), lambda b, c, t: (0, c)),
            pl.BlockSpec((2, nb, RG_BLOCK, RG_BLOCK), lambda b, c, t: (0, c, 0, 0)),
            pl.BlockSpec((2, tc), lambda b, c, t: (0, c)),
            pl.BlockSpec((1, tc), lambda b, c, t: (0, c)),
        ],
        out_specs=[
            pl.BlockSpec((None, tt, tc), lambda b, c, t: (b, t, c)),
            pl.BlockSpec((None, CONV_W - 1, tc), lambda b, c, t: (b, 0, c)),
            pl.BlockSpec((None, 1, tc), lambda b, c, t: (b, 0, c)),
        ],
        out_shape=[
            jax.ShapeDtypeStruct((bsz, t_len, d), BF16),
            jax.ShapeDtypeStruct((bsz, CONV_W - 1, d), F32),
            jax.ShapeDtypeStruct((bsz, 1, d), F32),
        ],
        scratch_shapes=[pltpu.VMEM((tt + 8, tc), F32), pltpu.VMEM((1, tc), F32)],
        compiler_params=_params("parallel", "parallel", "arbitrary"),
        name="rg_core",
    )(proj, proj, conv_state, h_state, conv_w, conv_b.reshape(1, d), gate_w,
      gate_b, lam.reshape(1, d))


def _lambda_init(layer_idx):
    return 0.8 - 0.6 * math.exp(-0.3 * layer_idx)


def _diff_lambda(dl, lam_init):
    s1 = jnp.sum(dl[0:1, :] * dl[1:2, :], axis=-1, keepdims=True)
    s2 = jnp.sum(dl[2:3, :] * dl[3:4, :], axis=-1, keepdims=True)
    return jnp.exp(s1) - jnp.exp(s2) + lam_init


def _split_components(q):
    lane = lax.broadcasted_iota(jnp.int32, q.shape, 1)
    lo = lane < HEAD_DIM
    zero = jnp.zeros_like(q)
    return jnp.where(lo, q, zero), jnp.where(lo, zero, q)


def _sub_norm(o, sub_g, lam_init):
    ms = jnp.mean(o * o, axis=-1, keepdims=True)
    return o * lax.rsqrt(ms + EPS) * sub_g * (1.0 - lam_init)


def _attn_prompt_kernel(slopes_ref, dl_ref, sg_ref, q_ref, k_ref, v_ref, km_ref, vm_ref, o_ref,
                        *, tq, lam_init):
    h = pl.program_id(0)
    i = pl.program_id(1)
    slope = slopes_ref[h]
    qs = _split_components(q_ref[...])

    r = lax.broadcasted_iota(jnp.int32, (tq, tq), 0)
    c = lax.broadcasted_iota(jnp.int32, (tq, tq), 1)
    dbase = (r - c).astype(F32)
    nb_full = -slope * dbase
    nb_diag = -slope * jnp.abs(dbase)
    shift = CHUNK.bit_length() - 1
    mask_diag = (c >> shift) <= (r >> shift)

    rm = lax.broadcasted_iota(jnp.int32, (tq, N_META), 0)
    cm = lax.broadcasted_iota(jnp.int32, (tq, N_META), 1)
    dist_m = (N_META + i * tq + rm - cm).astype(F32)
    bias_m = -slope * dist_m
    km = km_ref[...]
    vm = vm_ref[...]
    state = []
    for comp in range(2):
        u = _dot_nt(qs[comp], km) + bias_m
        m0 = jnp.max(u, axis=-1, keepdims=True)
        p = jnp.exp(u - m0)
        l0 = jnp.sum(p, axis=-1, keepdims=True)
        acc = _dot(p.astype(BF16), vm)
        state += [m0, l0, acc]

    def step(u, vc, m_old, l_old, acc_old, cterm):
        mx = jnp.max(u, axis=-1, keepdims=True) + cterm
        m_new = jnp.maximum(m_old, mx)
        alpha = jnp.exp(m_old - m_new)
        p = jnp.exp(u - (m_new - cterm))
        l_new = alpha * l_old + jnp.sum(p, axis=-1, keepdims=True)
        acc_new = alpha * acc_old + _dot(p.astype(BF16), vc)
        return m_new, l_new, acc_new

    def full_tile(jj, carry):
        start = pl.multiple_of(jj * tq, tq)
        kc = k_ref[pl.ds(start, tq), :]
        vc = v_ref[pl.ds(start, tq), :]
        cterm = -slope * ((i - jj) * tq).astype(F32)
        out = []
        for comp in range(2):
            u = _dot_nt(qs[comp], kc) + nb_full
            out += list(step(u, vc, *carry[3 * comp:3 * comp + 3], cterm))
        return tuple(out)

    state = lax.fori_loop(0, i, full_tile, tuple(state))

    start = pl.multiple_of(i * tq, tq)
    kc = k_ref[pl.ds(start, tq), :]
    vc = v_ref[pl.ds(start, tq), :]
    outs = []
    for comp in range(2):
        u = jnp.where(mask_diag, _dot_nt(qs[comp], kc) + nb_diag, -jnp.inf)
        _, l_f, acc_f = step(u, vc, *state[3 * comp:3 * comp + 3], 0.0)
        outs.append(acc_f / l_f)
    lam = _diff_lambda(dl_ref[...], lam_init)
    o = outs[0] - lam * outs[1]
    o_ref[...] = _sub_norm(o, sg_ref[...], lam_init).astype(o_ref.dtype)


def _attn_prompt(q_hm, k_hm, v_hm, km_hm, vm_hm, slopes, dl, sub_g, *, tq, lam_init):
    nh, t_len, _ = q_hm.shape
    kern = functools.partial(_attn_prompt_kernel, tq=tq, lam_init=lam_init)
    return pl.pallas_call(
        kern,
        grid=(nh, t_len // tq),
        in_specs=[
            pl.BlockSpec(memory_space=pltpu.SMEM),
            pl.BlockSpec((4, HEAD_DIM), lambda h, i: (0, 0)),
            pl.BlockSpec((1, LANES), lambda h, i: (0, 0)),
            pl.BlockSpec((None, tq, LANES), lambda h, i: (h, i, 0)),
            pl.BlockSpec((None, t_len, LANES), lambda h, i: (h, 0, 0)),
            pl.BlockSpec((None, t_len, LANES), lambda h, i: (h, 0, 0)),
            pl.BlockSpec((None, N_META, LANES), lambda h, i: (h, 0, 0)),
            pl.BlockSpec((None, N_META, LANES), lambda h, i: (h, 0, 0)),
        ],
        out_specs=pl.BlockSpec((tq, LANES), lambda h, i: (i, h)),
        out_shape=jax.ShapeDtypeStruct((t_len, nh * LANES), BF16),
        compiler_params=_params("parallel", "arbitrary"),
        name="attn_prompt",
    )(slopes, dl, sub_g.reshape(1, LANES), q_hm, k_hm, v_hm, km_hm, vm_hm)


def _attn_sample_kernel(slopes_ref, dl_ref, sg_ref, q_ref, kc_ref, vc_ref, kn_ref, vn_ref, km_ref, vm_ref,
                        o_ref, *, heads, past, ts, lam_init):
    hg = pl.program_id(1)
    lam = _diff_lambda(dl_ref[...], lam_init)

    def dist(n_keys, key_pos0):
        rr = lax.broadcasted_iota(jnp.int32, (ts, n_keys), 0)
        cc = lax.broadcasted_iota(jnp.int32, (ts, n_keys), 1)
        return jnp.abs((N_META + past + rr) - (key_pos0 + cc)).astype(F32)

    d_meta = dist(N_META, 0)
    d_cache = dist(past, N_META)
    d_new = dist(ts, N_META + past)

    for hh in range(heads):
        slope = slopes_ref[hg * heads + hh]
        sl = slice(hh * LANES, (hh + 1) * LANES)
        qs = _split_components(q_ref[hh])
        kc = kc_ref[:, sl].astype(BF16)
        vc = vc_ref[:, sl].astype(BF16)
        kn, vn, km, vm = kn_ref[hh], vn_ref[hh], km_ref[hh], vm_ref[hh]
        ws = []
        for comp in range(2):
            u_m = _dot_nt(qs[comp], km) - slope * d_meta
            u_c = _dot_nt(qs[comp], kc) - slope * d_cache
            u_n = _dot_nt(qs[comp], kn) - slope * d_new
            mx = jnp.maximum(jnp.maximum(jnp.max(u_m, axis=-1, keepdims=True),
                                         jnp.max(u_c, axis=-1, keepdims=True)),
                             jnp.max(u_n, axis=-1, keepdims=True))
            p_m, p_c, p_n = jnp.exp(u_m - mx), jnp.exp(u_c - mx), jnp.exp(u_n - mx)
            inv = 1.0 / (jnp.sum(p_m, axis=-1, keepdims=True) + jnp.sum(p_c, axis=-1, keepdims=True)
                         + jnp.sum(p_n, axis=-1, keepdims=True))
            ws.append((p_m * inv, p_c * inv, p_n * inv))
        w_m, w_c, w_n = [ws[0][k] - lam * ws[1][k] for k in range(3)]
        o = _dot(w_m.astype(BF16), vm) + _dot(w_c.astype(BF16), vc) + _dot(w_n.astype(BF16), vn)
        o_ref[:, sl] = _sub_norm(o, sg_ref[...], lam_init).astype(o_ref.dtype)


def _attn_sample(q_hm, cache_k, cache_v, k_hm, v_hm, slopes, dl, sub_g, *, bs, ts, lam_init, heads=4):
    past = cache_k.shape[1]
    tn = heads * LANES
    meta_blk = (bs * ts) // N_META
    kern = functools.partial(_attn_sample_kernel, heads=heads, past=past, ts=ts, lam_init=lam_init)
    return pl.pallas_call(
        kern,
        grid=(bs, N_HEADS // heads),
        in_specs=[
            pl.BlockSpec(memory_space=pltpu.SMEM),
            pl.BlockSpec((4, HEAD_DIM), lambda b, g: (0, 0)),
            pl.BlockSpec((1, LANES), lambda b, g: (0, 0)),
            pl.BlockSpec((heads, ts, LANES), lambda b, g: (g, b, 0)),
            pl.BlockSpec((None, past, tn), lambda b, g: (b, 0, g)),
            pl.BlockSpec((None, past, tn), lambda b, g: (b, 0, g)),
            pl.BlockSpec((heads, ts, LANES), lambda b, g: (g, b, 0)),
            pl.BlockSpec((heads, ts, LANES), lambda b, g: (g, b, 0)),
            pl.BlockSpec((heads, N_META, LANES), lambda b, g: (g, meta_blk, 0)),
            pl.BlockSpec((heads, N_META, LANES), lambda b, g: (g, meta_blk, 0)),
        ],
        out_specs=pl.BlockSpec((ts, tn), lambda b, g: (b, g)),
        out_shape=jax.ShapeDtypeStruct((bs * ts, N_HEADS * LANES), BF16),
        compiler_params=_params("parallel", "arbitrary"),
        name="attn_sample",
    )(slopes, dl, sub_g.reshape(1, LANES), q_hm, cache_k, cache_v, k_hm, v_hm, k_hm, v_hm)


def _pick(n, pref):
    for t in pref:
        if n % t == 0:
            return t
    return n


def kernel(x_prompt, x_sample, cache_k, cache_v, state_conv, state_h, meta_tokens, ffn_norm, ffn_w_gate, ffn_w_up, ffn_w_down, rg_norm, rg_w_in, rg_conv_w, rg_conv_b, rg_gate_w, rg_gate_b, rg_lambda, rg_w_out, kv_norm, w_kv, k_norm, attn_norm, w_q, q_norm, diff_lambda, sub_norm, w_o):
    bp, tp, d = x_prompt.shape
    bs, ts, _ = x_sample.shape
    past = cache_k.shape[1]
    depth = ffn_norm.shape[0]
    assert bp == 1 and depth == 2 and rg_norm.shape[0] == 1 and attn_norm.shape[0] == 1
    assert ts == N_META and past % CHUNK == 0 and ts <= CHUNK
    d_rnn = rg_w_out.shape[1]

    wg = ffn_w_gate.astype(BF16)
    wu = ffn_w_up.astype(BF16)
    wd = ffn_w_down.astype(BF16)
    w_in = rg_w_in[0].astype(BF16)
    w_out = rg_w_out[0].astype(BF16)
    gate_w = rg_gate_w[0].astype(BF16)
    w_kv_b = w_kv.astype(BF16)
    w_q_b = w_q[0].astype(BF16)
    w_o_b = w_o[0].astype(BF16)
    slopes = 2.0 ** (-8.0 * jnp.arange(1, N_HEADS + 1, dtype=F32) / N_HEADS)
    lam_init = _lambda_init(1)

    n_small = bs * ts + N_META
    tm_p = _pick(tp, (512, 256, 128))
    tt_p = _pick(tp, (256, 128))
    tq_p = _pick(tp, (512, 256, 128))

    def ffn(x, l, s, tm):
        return _ffn_half(x, ffn_norm[l, s], wg[l, s], wu[l, s], wd[l, s], tm=tm, tf=512)

    x_small = jnp.concatenate([x_sample.reshape(bs * ts, d), meta_tokens], axis=0)
    x_small = ffn(x_small, 0, 0, n_small)
    proj_s = _norm_matmul(x_small, rg_norm[0], w_in, tm=n_small, tn=1024)
    conv0 = jnp.concatenate([state_conv[0], jnp.zeros((1, CONV_W - 1, d_rnn), F32)], axis=0)
    h0 = jnp.concatenate([state_h[0], jnp.zeros((1, d_rnn), F32)], axis=0).reshape(bs + 1, 1, d_rnn)
    z_s, nconv_s, nh_s = _rg_core(proj_s.reshape(bs + 1, ts, 2 * d_rnn), conv0, h0, rg_conv_w[0], rg_conv_b[0],
                                  gate_w, rg_gate_b[0], rg_lambda[0], tt=ts, tc=512)
    x_small = _matmul_res(z_s.reshape(n_small, d_rnn), w_out, x_small, tm=n_small, tn=1024)
    x_small = ffn(x_small, 0, 1, n_small)
    k_s, v_s, khm_s, vhm_s = _kv_proj(x_small, kv_norm, w_kv_b, k_norm, tm=n_small)

    xp = x_prompt.reshape(tp, d)
    xp = ffn(xp, 0, 0, tm_p)
    proj_p = _norm_matmul(xp, rg_norm[0], w_in, tm=tm_p, tn=1024)
    z_p, nconv_p, nh_p = _rg_core(proj_p.reshape(1, tp, 2 * d_rnn), nconv_s[bs:], nh_s[bs:], rg_conv_w[0],
                                  rg_conv_b[0], gate_w, rg_gate_b[0], rg_lambda[0], tt=tt_p, tc=512)
    xp = _matmul_res(z_p.reshape(tp, d_rnn), w_out, xp, tm=tm_p, tn=1024)
    xp = ffn(xp, 0, 1, tm_p)
    k_p, v_p, khm_p, vhm_p = _kv_proj(xp, kv_norm, w_kv_b, k_norm, tm=tm_p)

    xp = ffn(xp, 1, 0, tm_p)
    q_p = _q_proj(xp, attn_norm[0], w_q_b, q_norm[0], tm=tm_p)
    o_p = _attn_prompt(q_p, khm_p, vhm_p, khm_s[:, bs * ts:], vhm_s[:, bs * ts:], slopes, diff_lambda[0],
                       sub_norm[0], tq=tq_p, lam_init=lam_init)
    xp = _matmul_res(o_p, w_o_b, xp, tm=tm_p, tn=1024)
    xp = ffn(xp, 1, 1, tm_p)

    xs = x_small[:bs * ts]
    xs = ffn(xs, 1, 0, bs * ts)
    q_s = _q_proj(xs, attn_norm[0], w_q_b, q_norm[0], tm=bs * ts)
    o_s = _attn_sample(q_s, cache_k.reshape(bs, past, -1), cache_v.reshape(bs, past, -1), khm_s, vhm_s,
                       slopes, diff_lambda[0], sub_norm[0], bs=bs, ts=ts, lam_init=lam_init)
    xs = _matmul_res(o_s, w_o_b, xs, tm=bs * ts, tn=1024)
    xs = ffn(xs, 1, 1, bs * ts)

    n_s = bs * ts
    k_all = jnp.concatenate([k_s[n_s:], k_p], axis=0).reshape(1, N_META + tp, N_HEADS, 2, HEAD_DIM)
    v_all = jnp.concatenate([v_s[n_s:], v_p], axis=0).reshape(1, N_META + tp, N_HEADS, V_HEAD_DIM)
    return (
        xp.reshape(1, tp, d),
        xs.reshape(bs, ts, d),
        k_all,
        v_all,
        nconv_p.reshape(1, 1, CONV_W - 1, d_rnn),
        nh_p.reshape(1, 1, d_rnn),
        k_s[:n_s].reshape(bs, ts, N_HEADS, 2, HEAD_DIM),
        v_s[:n_s].reshape(bs, ts, N_HEADS, V_HEAD_DIM),
        nconv_s[:bs].reshape(1, bs, CONV_W - 1, d_rnn),
        nh_s[:bs].reshape(1, bs, d_rnn),
    )
```

```python
import functools
import math

import jax
import jax.numpy as jnp
from jax import lax
from jax.experimental import pallas as pl
from jax.experimental.pallas import tpu as pltpu

F32 = jnp.float32
BF16 = jnp.bfloat16

EPS = 1e-6
CHUNK = 64
N_META = 16
CONV_W = 4
LRU_C = 8.0
N_HEADS = 16
HEAD_DIM = 64
V_HEAD_DIM = 2 * HEAD_DIM
RG_BLOCK = 128
LANES = 128
VMEM_LIMIT = 56 * 1024 * 1024
LOG2E = math.log2(math.e)
Q_SCALE = HEAD_DIM ** -0.5 * LOG2E
N_SLOPE_PIECES = 3
POS_LANE0 = HEAD_DIM
POS_LO_LANE0 = HEAD_DIM + N_SLOPE_PIECES
POS_LANE_END = HEAD_DIM + 2 * N_SLOPE_PIECES
POS_SPLIT_BITS = 8
L_COL = V_HEAD_DIM
ROW_BLOCK = 64


def _params(*sem):
    return pltpu.CompilerParams(dimension_semantics=sem, vmem_limit_bytes=VMEM_LIMIT)


def _rms_rows(x, g):
    ms = jnp.mean(x * x, axis=-1, keepdims=True)
    return x * lax.rsqrt(ms + EPS) * g


def _dot(a, b):
    return jnp.dot(a, b, preferred_element_type=F32)


def _dot_nt(a, b):
    return lax.dot_general(a, b, (((1,), (1,)), ((), ())), preferred_element_type=F32)


def _ffn_kernel(x_ref, g_ref, wg_ref, wu_ref, wd_ref, o_ref, u_ref):
    j = pl.program_id(1)

    @pl.when(j == 0)
    def _():
        x = x_ref[...]
        u_ref[...] = _rms_rows(x, g_ref[...]).astype(BF16)
        o_ref[...] = x

    u = u_ref[...]
    a = _dot(u, wg_ref[...])
    b = _dot(u, wu_ref[...])
    h = (a * jax.nn.sigmoid(a)) * b
    o_ref[...] += 0.5 * _dot(h.astype(BF16), wd_ref[...])


def _ffn_half(x, g, wg, wu, wd, *, tm, tf):
    m, d = x.shape
    dff = wg.shape[1]
    return pl.pallas_call(
        _ffn_kernel,
        grid=(m // tm, dff // tf),
        in_specs=[
            pl.BlockSpec((tm, d), lambda i, j: (i, 0)),
            pl.BlockSpec((1, d), lambda i, j: (0, 0)),
            pl.BlockSpec((d, tf), lambda i, j: (0, j)),
            pl.BlockSpec((d, tf), lambda i, j: (0, j)),
            pl.BlockSpec((tf, d), lambda i, j: (j, 0)),
        ],
        out_specs=pl.BlockSpec((tm, d), lambda i, j: (i, 0)),
        out_shape=jax.ShapeDtypeStruct((m, d), F32),
        scratch_shapes=[pltpu.VMEM((tm, d), BF16)],
        compiler_params=_params("parallel", "arbitrary"),
        name="ffn_half",
    )(x, g.reshape(1, d), wg, wu, wd)


def _norm_matmul_kernel(x_ref, g_ref, w_ref, o_ref, u_ref):
    @pl.when(pl.program_id(1) == 0)
    def _():
        u_ref[...] = _rms_rows(x_ref[...], g_ref[...]).astype(BF16)

    o_ref[...] = _dot(u_ref[...], w_ref[...])


def _norm_matmul(x, g, w, *, tm, tn):
    m, d = x.shape
    n = w.shape[1]
    return pl.pallas_call(
        _norm_matmul_kernel,
        grid=(m // tm, n // tn),
        in_specs=[
            pl.BlockSpec((tm, d), lambda i, j: (i, 0)),
            pl.BlockSpec((1, d), lambda i, j: (0, 0)),
            pl.BlockSpec((d, tn), lambda i, j: (0, j)),
        ],
        out_specs=pl.BlockSpec((tm, tn), lambda i, j: (i, j)),
        out_shape=jax.ShapeDtypeStruct((m, n), F32),
        scratch_shapes=[pltpu.VMEM((tm, d), BF16)],
        compiler_params=_params("parallel", "arbitrary"),
        name="norm_matmul",
    )(x, g.reshape(1, d), w)


def _lane_lo(shape):
    return lax.broadcasted_iota(jnp.int32, shape, 1) < HEAD_DIM


def _head_rms(y, g_row):
    lo = _lane_lo(y.shape)
    sq = y * y
    s_lo = jnp.sum(jnp.where(lo, sq, 0.0), axis=-1, keepdims=True)
    s_hi = jnp.sum(jnp.where(lo, 0.0, sq), axis=-1, keepdims=True)
    inv = jnp.where(lo, lax.rsqrt(s_lo * (1.0 / HEAD_DIM) + EPS), lax.rsqrt(s_hi * (1.0 / HEAD_DIM) + EPS))
    return y * inv * g_row


def _split_slabs(y, extra):
    lo = _lane_lo(y.shape)
    swapped = pltpu.roll(y, HEAD_DIM, axis=1)
    return jnp.where(lo, y, extra), jnp.where(lo, swapped, extra)


def _kv_kernel(x_ref, g_ref, wk_ref, wv_ref, kg_ref, *refs, heads, packed, split, pos_mod):
    k_ref, v_ref = refs[0], refs[1]
    rest = list(refs[2:])
    if packed:
        khm_ref, vhm_ref = rest[0], rest[1]
        rest = rest[2:]
    if split:
        k0_ref, k1_ref, va_ref = rest[0], rest[1], rest[2]
        rest = rest[3:]
    (u_ref,) = rest

    @pl.when(pl.program_id(1) == 0)
    def _():
        u_ref[...] = _rms_rows(x_ref[...], g_ref[...]).astype(BF16)

    u = u_ref[...]
    k = _dot(u, wk_ref[...])
    v = _dot(u, wv_ref[...])
    v_ref[...] = v
    tm = k.shape[0]
    if split:
        lane = lax.broadcasted_iota(jnp.int32, (tm, LANES), 1)
        row = lax.broadcasted_iota(jnp.int32, (tm, LANES), 0) + pl.program_id(0) * tm
        pos = row & (pos_mod - 1)
        pos_lo = pos & ((1 << POS_SPLIT_BITS) - 1)
        pos_hi = pos - pos_lo
        pos_lanes = jnp.where((lane >= POS_LANE0) & (lane < POS_LO_LANE0), pos_hi,
                              jnp.where((lane >= POS_LO_LANE0) & (lane < POS_LANE_END), pos_lo, 0)).astype(F32)
        ones_col = jnp.where(lane == 0, 1.0, 0.0).astype(BF16)
    for hh in range(heads):
        sl = slice(hh * LANES, (hh + 1) * LANES)
        kh = _head_rms(k[:, sl], kg_ref[...])
        k_ref[:, sl] = kh
        if packed:
            khm_ref[hh] = kh.astype(BF16)
            vhm_ref[hh] = v[:, sl].astype(BF16)
        if split:
            k0, k1 = _split_slabs(kh, pos_lanes)
            k0_ref[hh] = k0.astype(BF16)
            k1_ref[hh] = k1.astype(BF16)
            va_ref[hh, :, 0:V_HEAD_DIM] = v[:, sl].astype(BF16)
            va_ref[hh, :, V_HEAD_DIM:2 * V_HEAD_DIM] = ones_col


def _kv_proj(x, g, w_kv, k_g, *, tm, packed, split, pos_mod, heads=4):
    m, d = x.shape
    qk = w_kv.shape[1] // 2
    tn = heads * LANES
    nj = qk // tn
    kg_row = jnp.tile(k_g, 2).reshape(1, LANES)
    hm_spec = pl.BlockSpec((heads, tm, LANES), lambda i, j: (j, i, 0))
    hm_shape = jax.ShapeDtypeStruct((N_HEADS, m, LANES), BF16)
    out_specs = [pl.BlockSpec((tm, tn), lambda i, j: (i, j))] * 2
    out_shape = [jax.ShapeDtypeStruct((m, qk), F32)] * 2
    if packed:
        out_specs += [hm_spec, hm_spec]
        out_shape += [hm_shape, hm_shape]
    if split:
        out_specs += [hm_spec, hm_spec, pl.BlockSpec((heads, tm, 2 * V_HEAD_DIM), lambda i, j: (j, i, 0))]
        out_shape += [hm_shape, hm_shape, jax.ShapeDtypeStruct((N_HEADS, m, 2 * V_HEAD_DIM), BF16)]
    return pl.pallas_call(
        functools.partial(_kv_kernel, heads=heads, packed=packed, split=split, pos_mod=pos_mod),
        grid=(m // tm, nj),
        in_specs=[
            pl.BlockSpec((tm, d), lambda i, j: (i, 0)),
            pl.BlockSpec((1, d), lambda i, j: (0, 0)),
            pl.BlockSpec((d, tn), lambda i, j: (0, j)),
            pl.BlockSpec((d, tn), lambda i, j: (0, j + nj)),
            pl.BlockSpec((1, LANES), lambda i, j: (0, 0)),
        ],
        out_specs=out_specs,
        out_shape=out_shape,
        scratch_shapes=[pltpu.VMEM((tm, d), BF16)],
        compiler_params=_params("parallel", "arbitrary"),
        name="kv_proj",
    )(x, g.reshape(1, d), w_kv, w_kv, kg_row)


def _q_kernel(x_ref, g_ref, w_ref, qg_ref, ex_ref, *refs, heads, split):
    u_ref = refs[-1]

    @pl.when(pl.program_id(1) == 0)
    def _():
        u_ref[...] = _rms_rows(x_ref[...], g_ref[...]).astype(BF16)

    q = _dot(u_ref[...], w_ref[...])
    for hh in range(heads):
        sl = slice(hh * LANES, (hh + 1) * LANES)
        qh = _head_rms(q[:, sl], qg_ref[...]) * Q_SCALE
        if split:
            qa, qb = _split_slabs(qh, ex_ref[hh])
            refs[0][hh] = qa.astype(BF16)
            refs[1][hh] = qb.astype(BF16)
        else:
            refs[0][hh] = qh.astype(BF16)


def _q_proj(x, g, w_q, q_g, slope_lanes, *, tm, split, heads=4):
    m, d = x.shape
    n = w_q.shape[1]
    tn = heads * LANES
    qg_row = jnp.tile(q_g, 2).reshape(1, LANES)
    n_out = 2 if split else 1
    return pl.pallas_call(
        functools.partial(_q_kernel, heads=heads, split=split),
        grid=(m // tm, n // tn),
        in_specs=[
            pl.BlockSpec((tm, d), lambda i, j: (i, 0)),
            pl.BlockSpec((1, d), lambda i, j: (0, 0)),
            pl.BlockSpec((d, tn), lambda i, j: (0, j)),
            pl.BlockSpec((1, LANES), lambda i, j: (0, 0)),
            pl.BlockSpec((heads, 1, LANES), lambda i, j: (j, 0, 0)),
        ],
        out_specs=[pl.BlockSpec((heads, tm, LANES), lambda i, j: (j, i, 0))] * n_out,
        out_shape=[jax.ShapeDtypeStruct((N_HEADS, m, LANES), BF16)] * n_out,
        scratch_shapes=[pltpu.VMEM((tm, d), BF16)],
        compiler_params=_params("parallel", "arbitrary"),
        name="q_proj",
    )(x, g.reshape(1, d), w_q, qg_row, slope_lanes)


def _matmul_res_kernel(z_ref, w_ref, x_ref, o_ref):
    o_ref[...] = x_ref[...] + _dot(z_ref[...], w_ref[...])


def _matmul_res(z, w, x, *, tm, tn):
    m, kdim = z.shape
    n = w.shape[1]
    return pl.pallas_call(
        _matmul_res_kernel,
        grid=(m // tm, n // tn),
        in_specs=[
            pl.BlockSpec((tm, kdim), lambda i, j: (i, 0)),
            pl.BlockSpec((kdim, tn), lambda i, j: (0, j)),
            pl.BlockSpec((tm, tn), lambda i, j: (i, j)),
        ],
        out_specs=pl.BlockSpec((tm, tn), lambda i, j: (i, j)),
        out_shape=jax.ShapeDtypeStruct((m, n), F32),
        compiler_params=_params("parallel", "arbitrary"),
        name="matmul_res",
    )(z, w, x)


def _expm1_nonpos(x):
    u = jnp.exp(x)
    near = x > -0.5
    safe_log = jnp.log(jnp.where(near, u, 0.5))
    stable = jnp.where(u == 1.0, x, (u - 1.0) * x / safe_log)
    return jnp.where(near, stable, u - 1.0)


def _rg_kernel(gate_ref, xb_ref, cs_ref, hs_ref, cw_ref, cb_ref, gw_ref, gb_ref, lam_ref,
               z_ref, nc_ref, nh_ref, xs_ref, hc_ref, *, tt, tc):
    t = pl.program_id(2)
    pad = 8

    @pl.when(t == 0)
    def _():
        xs_ref[pad - (CONV_W - 1):pad, :] = cs_ref[...]
        hc_ref[...] = hs_ref[...]

    xs_ref[pad:pad + tt, :] = xb_ref[...]
    cw = cw_ref[...]
    xc = cb_ref[...] + cw[CONV_W - 1:CONV_W, :] * xb_ref[...]
    for k in range(CONV_W - 1):
        off = pad - (CONV_W - 1) + k
        xc = xc + cw[k:k + 1, :] * xs_ref[off:off + tt, :]

    gb = gb_ref[...]
    g_r, g_i = [], []
    for n in range(tc // RG_BLOCK):
        sl = slice(n * RG_BLOCK, (n + 1) * RG_BLOCK)
        xcb = xc[:, sl].astype(BF16)
        g_r.append(_dot(xcb, gw_ref[0, n]))
        g_i.append(_dot(xcb, gw_ref[1, n]))
    g_r = jnp.concatenate(g_r, axis=1) + gb[0:1, :]
    g_i = jnp.concatenate(g_i, axis=1) + gb[1:2, :]
    r = jax.nn.sigmoid(g_r)
    ig = jax.nn.sigmoid(g_i)
    lam = lam_ref[...]
    log_sig = jnp.minimum(lam, 0.0) - jnp.log1p(jnp.exp(-jnp.abs(lam)))
    log_a = LRU_C * r * log_sig
    a = jnp.exp(log_a)
    b = jnp.sqrt(-_expm1_nonpos(2.0 * log_a)) * (ig * xc)

    row = lax.broadcasted_iota(jnp.int32, (tt, tc), 0)
    d = 1
    while d < tt:
        keep = row >= d
        a_sh = pltpu.roll(a, d, axis=0)
        b_sh = pltpu.roll(b, d, axis=0)
        b = jnp.where(keep, a * b_sh + b, b)
        a = jnp.where(keep, a * a_sh, a)
        d *= 2
    h = a * hc_ref[...] + b

    z_ref[...] = (jax.nn.gelu(gate_ref[...]) * h).astype(z_ref.dtype)
    hc_ref[...] = h[tt - 1:tt, :]
    xs_ref[0:pad, :] = xs_ref[tt:tt + pad, :]

    @pl.when(t == pl.num_programs(2) - 1)
    def _():
        nc_ref[...] = xs_ref[tt + pad - (CONV_W - 1):tt + pad, :]
        nh_ref[...] = h[tt - 1:tt, :]


def _rg_core(proj, conv_state, h_state, conv_w, conv_b, gate_w, gate_b, lam, *, tt, tc):
    bsz, t_len, two_d = proj.shape
    d = two_d // 2
    nc = d // tc
    nb = tc // RG_BLOCK
    kern = functools.partial(_rg_kernel, tt=tt, tc=tc)
    return pl.pallas_call(
        kern,
        grid=(bsz, nc, t_len // tt),
        in_specs=[
            pl.BlockSpec((None, tt, tc), lambda b, c, t: (b, t, c)),
            pl.BlockSpec((None, tt, tc), lambda b, c, t: (b, t, c + nc)),
            pl.BlockSpec((None, CONV_W - 1, tc), lambda b, c, t: (b, 0, c)),
            pl.BlockSpec((None, 1, tc), lambda b, c, t: (b, 0, c)),
            pl.BlockSpec((CONV_W, tc), lambda b, c, t: (0, c)),
            pl.BlockSpec((1, tc), lambda b, c, t: (0, c)),
            pl.BlockSpec((2, nb, RG_BLOCK, RG_BLOCK), lambda b, c, t: (0, c, 0, 0)),
            pl.BlockSpec((2, tc), lambda b, c, t: (0, c)),
            pl.BlockSpec((1, tc), lambda b, c, t: (0, c)),
        ],
        out_specs=[
            pl.BlockSpec((None, tt, tc), lambda b, c, t: (b, t, c)),
            pl.BlockSpec((None, CONV_W - 1, tc), lambda b, c, t: (b, 0, c)),
            pl.BlockSpec((None, 1, tc), lambda b, c, t: (b, 0, c)),
        ],
        out_shape=[
            jax.ShapeDtypeStruct((bsz, t_len, d), BF16),
            jax.ShapeDtypeStruct((bsz, CONV_W - 1, d), F32),
            jax.ShapeDtypeStruct((bsz, 1, d), F32),
        ],
        scratch_shapes=[pltpu.VMEM((tt + 8, tc), F32), pltpu.VMEM((1, tc), F32)],
        compiler_params=_params("parallel", "parallel", "arbitrary"),
        name="rg_core",
    )(proj, proj, conv_state, h_state, conv_w, conv_b.reshape(1, d), gate_w,
      gate_b, lam.reshape(1, d))


def _lambda_init(layer_idx):
    return 0.8 - 0.6 * math.exp(-0.3 * layer_idx)


def _diff_lambda(dl, lam_init):
    s1 = jnp.sum(dl[0:1, :] * dl[1:2, :], axis=-1, keepdims=True)
    s2 = jnp.sum(dl[2:3, :] * dl[3:4, :], axis=-1, keepdims=True)
    return jnp.exp(s1) - jnp.exp(s2) + lam_init


def _sub_norm(o, sub_g, lam_init):
    ms = jnp.mean(o * o, axis=-1, keepdims=True)
    return o * lax.rsqrt(ms + EPS) * sub_g * (1.0 - lam_init)


def _attn_prompt_kernel(slopes_ref, dl_ref, sg_ref, qa_ref, qb_ref, k0_ref, k1_ref, v_ref,
                        k0m_ref, k1m_ref, vm_ref, o_ref, sa_ref, sb_ref, p_ref, acc_ref, m_ref, alpha_ref,
                        *, tq, lam_init):
    h = pl.program_id(0)
    i = pl.program_id(1)
    slope = slopes_ref[h]
    q_aug = (qa_ref[...], qb_ref[...])
    lo = _lane_lo((tq, LANES))
    q_plain = tuple(jnp.where(lo, q, jnp.zeros_like(q)) for q in q_aug)
    k_refs = (k0_ref, k1_ref)
    km_refs = (k0m_ref, k1m_ref)

    cm = lax.broadcasted_iota(jnp.int32, (1, N_META), 1)
    bias_m = -slope * (N_META - cm + i * tq).astype(F32)
    vm = vm_ref[...]
    for comp in range(2):
        u = _dot_nt(q_plain[comp], km_refs[comp][...]) + bias_m
        m0 = jnp.max(u, axis=-1, keepdims=True)
        p = jnp.exp2(u - m0)
        m_ref[comp] = jnp.broadcast_to(m0, (tq, LANES))
        acc_ref[comp] = _dot(p.astype(BF16), vm)

    def scores(jj, dst_ref):
        start = pl.multiple_of(jj * tq, tq)
        for comp in range(2):
            dst_ref[comp] = _dot_nt(q_aug[comp], k_refs[comp][pl.ds(start, tq), :])

    shift = CHUNK.bit_length() - 1
    n_cb = tq // LANES

    def softmax_rows(s_ref, comp, cterm, diag):
        for rb in range(tq // ROW_BLOCK):
            rows = slice(rb * ROW_BLOCK, (rb + 1) * ROW_BLOCK)
            blocks = []
            for cb in range(n_cb):
                s2 = s_ref[comp, rows, cb * LANES:(cb + 1) * LANES]
                if diag:
                    r = rb * ROW_BLOCK + lax.broadcasted_iota(jnp.int32, (ROW_BLOCK, LANES), 0)
                    c = cb * LANES + lax.broadcasted_iota(jnp.int32, (ROW_BLOCK, LANES), 1)
                    s2 = s2 + jnp.where(c > r, (2.0 * slope) * (r - c).astype(F32), 0.0)
                    s2 = jnp.where((c >> shift) <= (r >> shift), s2, -jnp.inf)
                blocks.append(s2)
            mx = functools.reduce(jnp.maximum, blocks)
            mx = jnp.max(mx, axis=-1, keepdims=True) + cterm
            m_old = m_ref[comp, rows, :]
            m_new = jnp.maximum(m_old, mx)
            alpha_ref[comp, rows, :] = jnp.exp2(m_old - m_new)
            m_ref[comp, rows, :] = m_new
            sub = m_new - cterm
            for cb in range(n_cb):
                p_ref[comp, rows, cb * LANES:(cb + 1) * LANES] = jnp.exp2(blocks[cb] - sub).astype(BF16)

    def accumulate(comp, vc):
        pv = _dot(p_ref[comp], vc)
        alpha = alpha_ref[comp]
        for half in range(2):
            cols = slice(half * LANES, (half + 1) * LANES)
            acc_ref[comp, :, cols] = alpha * acc_ref[comp, :, cols] + pv[:, cols]

    scores(0, sa_ref)

    def tile_step(jj, src_ref, dst_ref):
        scores(jj + 1, dst_ref)
        start = pl.multiple_of(jj * tq, tq)
        vc = v_ref[pl.ds(start, tq), :]
        cterm = slope * ((jj - i) * tq).astype(F32)
        for comp in range(2):
            softmax_rows(src_ref, comp, cterm, False)
            accumulate(comp, vc)

    def full_tile(jj, carry):
        @pl.when(jj % 2 == 0)
        def _():
            tile_step(jj, sa_ref, sb_ref)

        @pl.when(jj % 2 == 1)
        def _():
            tile_step(jj, sb_ref, sa_ref)

        return carry

    lax.fori_loop(0, i, full_tile, 0)

    def diag_tile(src_ref):
        start = pl.multiple_of(i * tq, tq)
        vc = v_ref[pl.ds(start, tq), :]
        for comp in range(2):
            softmax_rows(src_ref, comp, 0.0, True)
            accumulate(comp, vc)

    @pl.when(i % 2 == 0)
    def _():
        diag_tile(sa_ref)

    @pl.when(i % 2 == 1)
    def _():
        diag_tile(sb_ref)

    outs = []
    for comp in range(2):
        acc = acc_ref[comp]
        outs.append(acc[:, 0:V_HEAD_DIM] / acc[:, L_COL:L_COL + 1])
    lam = _diff_lambda(dl_ref[...], lam_init)
    o = outs[0] - lam * outs[1]
    o_ref[...] = _sub_norm(o, sg_ref[...], lam_init).astype(o_ref.dtype)


def _attn_prompt(qa, qb, k0, k1, va, k0m, k1m, vam, slopes, dl, sub_g, *, tq, lam_init):
    nh, t_len, _ = qa.shape
    kern = functools.partial(_attn_prompt_kernel, tq=tq, lam_init=lam_init)
    q_spec = pl.BlockSpec((None, tq, LANES), lambda h, i: (h, i, 0))
    k_spec = pl.BlockSpec((None, t_len, LANES), lambda h, i: (h, 0, 0))
    km_spec = pl.BlockSpec((None, N_META, LANES), lambda h, i: (h, 0, 0))
    return pl.pallas_call(
        kern,
        grid=(nh, t_len // tq),
        in_specs=[
            pl.BlockSpec(memory_space=pltpu.SMEM),
            pl.BlockSpec((4, HEAD_DIM), lambda h, i: (0, 0)),
            pl.BlockSpec((1, LANES), lambda h, i: (0, 0)),
            q_spec, q_spec, k_spec, k_spec,
            pl.BlockSpec((None, t_len, 2 * V_HEAD_DIM), lambda h, i: (h, 0, 0)),
            km_spec, km_spec,
            pl.BlockSpec((None, N_META, 2 * V_HEAD_DIM), lambda h, i: (h, 0, 0)),
        ],
        out_specs=pl.BlockSpec((tq, LANES), lambda h, i: (i, h)),
        out_shape=jax.ShapeDtypeStruct((t_len, nh * LANES), BF16),
        scratch_shapes=[
            pltpu.VMEM((2, tq, tq), F32),
            pltpu.VMEM((2, tq, tq), F32),
            pltpu.VMEM((2, tq, tq), BF16),
            pltpu.VMEM((2, tq, 2 * V_HEAD_DIM), F32),
            pltpu.VMEM((2, tq, LANES), F32),
            pltpu.VMEM((2, tq, LANES), F32),
        ],
        compiler_params=_params("parallel", "arbitrary"),
        name="attn_prompt",
    )(slopes, dl, sub_g.reshape(1, LANES), qa, qb, k0, k1, va, k0m, k1m, vam)


def _attn_sample_kernel(slopes_ref, dl_ref, sg_ref, q_ref, kc_ref, vc_ref, kn_ref, vn_ref, km_ref, vm_ref,
                        o_ref, *, heads, past, ts, lam_init):
    hg = pl.program_id(1)
    lam = _diff_lambda(dl_ref[...], lam_init)

    def dist(n_keys, key_pos0):
        rr = lax.broadcasted_iota(jnp.int32, (ts, n_keys), 0)
        cc = lax.broadcasted_iota(jnp.int32, (ts, n_keys), 1)
        return jnp.abs((N_META + past + rr) - (key_pos0 + cc)).astype(F32)

    d_meta = dist(N_META, 0)
    d_cache = dist(past, N_META)
    d_new = dist(ts, N_META + past)

    for hh in range(heads):
        slope = slopes_ref[hg * heads + hh]
        sl = slice(hh * LANES, (hh + 1) * LANES)
        q = q_ref[hh]
        lo = _lane_lo(q.shape)
        zero = jnp.zeros_like(q)
        qs = (jnp.where(lo, q, zero), jnp.where(lo, zero, q))
        kc = kc_ref[:, sl].astype(BF16)
        vc = vc_ref[:, sl].astype(BF16)
        kn, vn, km, vm = kn_ref[hh], vn_ref[hh], km_ref[hh], vm_ref[hh]
        ws = []
        for comp in range(2):
            u_m = _dot_nt(qs[comp], km) - slope * d_meta
            u_c = _dot_nt(qs[comp], kc) - slope * d_cache
            u_n = _dot_nt(qs[comp], kn) - slope * d_new
            mx = jnp.maximum(jnp.maximum(jnp.max(u_m, axis=-1, keepdims=True),
                                         jnp.max(u_c, axis=-1, keepdims=True)),
                             jnp.max(u_n, axis=-1, keepdims=True))
            p_m, p_c, p_n = jnp.exp2(u_m - mx), jnp.exp2(u_c - mx), jnp.exp2(u_n - mx)
            inv = 1.0 / (jnp.sum(p_m, axis=-1, keepdims=True) + jnp.sum(p_c, axis=-1, keepdims=True)
                         + jnp.sum(p_n, axis=-1, keepdims=True))
            ws.append((p_m * inv, p_c * inv, p_n * inv))
        w_m, w_c, w_n = [ws[0][k] - lam * ws[1][k] for k in range(3)]
        o = _dot(w_m.astype(BF16), vm) + _dot(w_c.astype(BF16), vc) + _dot(w_n.astype(BF16), vn)
        o_ref[:, sl] = _sub_norm(o, sg_ref[...], lam_init).astype(o_ref.dtype)


def _attn_sample(q_hm, cache_k, cache_v, k_hm, v_hm, slopes, dl, sub_g, *, bs, ts, lam_init, heads=4):
    past = cache_k.shape[1]
    tn = heads * LANES
    meta_blk = (bs * ts) // N_META
    kern = functools.partial(_attn_sample_kernel, heads=heads, past=past, ts=ts, lam_init=lam_init)
    return pl.pallas_call(
        kern,
        grid=(bs, N_HEADS // heads),
        in_specs=[
            pl.BlockSpec(memory_space=pltpu.SMEM),
            pl.BlockSpec((4, HEAD_DIM), lambda b, g: (0, 0)),
            pl.BlockSpec((1, LANES), lambda b, g: (0, 0)),
            pl.BlockSpec((heads, ts, LANES), lambda b, g: (g, b, 0)),
            pl.BlockSpec((None, past, tn), lambda b, g: (b, 0, g)),
            pl.BlockSpec((None, past, tn), lambda b, g: (b, 0, g)),
            pl.BlockSpec((heads, ts, LANES), lambda b, g: (g, b, 0)),
            pl.BlockSpec((heads, ts, LANES), lambda b, g: (g, b, 0)),
            pl.BlockSpec((heads, N_META, LANES), lambda b, g: (g, meta_blk, 0)),
            pl.BlockSpec((heads, N_META, LANES), lambda b, g: (g, meta_blk, 0)),
        ],
        out_specs=pl.BlockSpec((ts, tn), lambda b, g: (b, g)),
        out_shape=jax.ShapeDtypeStruct((bs * ts, N_HEADS * LANES), BF16),
        compiler_params=_params("parallel", "arbitrary"),
        name="attn_sample",
    )(slopes, dl, sub_g.reshape(1, LANES), q_hm, cache_k, cache_v, k_hm, v_hm, k_hm, v_hm)


def _pick(n, pref):
    for t in pref:
        if n % t == 0:
            return t
    return n


def _bf16_pieces(x, n):
    pieces = []
    rem = x
    for _ in range(n - 1):
        top = lax.bitcast_convert_type(lax.bitcast_convert_type(rem, jnp.uint32) & jnp.uint32(0xFFFF0000), F32)
        pieces.append(top)
        rem = rem - top
    pieces.append(rem)
    return pieces


def kernel(x_prompt, x_sample, cache_k, cache_v, state_conv, state_h, meta_tokens, ffn_norm, ffn_w_gate, ffn_w_up, ffn_w_down, rg_norm, rg_w_in, rg_conv_w, rg_conv_b, rg_gate_w, rg_gate_b, rg_lambda, rg_w_out, kv_norm, w_kv, k_norm, attn_norm, w_q, q_norm, diff_lambda, sub_norm, w_o):
    bp, tp, d = x_prompt.shape
    bs, ts, _ = x_sample.shape
    past = cache_k.shape[1]
    depth = ffn_norm.shape[0]
    assert bp == 1 and depth == 2 and rg_norm.shape[0] == 1 and attn_norm.shape[0] == 1
    assert ts == N_META and past % CHUNK == 0 and ts <= CHUNK
    d_rnn = rg_w_out.shape[1]

    wg = ffn_w_gate.astype(BF16)
    wu = ffn_w_up.astype(BF16)
    wd = ffn_w_down.astype(BF16)
    w_in = rg_w_in[0].astype(BF16)
    w_out = rg_w_out[0].astype(BF16)
    gate_w = rg_gate_w[0].astype(BF16)
    w_kv_b = w_kv.astype(BF16)
    w_q_b = w_q[0].astype(BF16)
    w_o_b = w_o[0].astype(BF16)
    slopes = (2.0 ** (-8.0 * jnp.arange(1, N_HEADS + 1, dtype=F32) / N_HEADS)) * LOG2E
    pieces = _bf16_pieces(slopes, N_SLOPE_PIECES)
    slope_lanes = jnp.zeros((N_HEADS, LANES), F32).at[:, POS_LANE0:POS_LANE_END].set(
        jnp.stack(pieces + pieces, axis=1)).reshape(N_HEADS, 1, LANES)
    lam_init = _lambda_init(1)

    n_s = bs * ts
    n_small = n_s + N_META
    tm_p = _pick(tp, (512, 256, 128))
    tt_p = _pick(tp, (256, 128))
    tq_p = _pick(tp, (512, 256, 128))
    assert tq_p % (1 << POS_SPLIT_BITS) == 0 and tq_p & (tq_p - 1) == 0

    def ffn(x, l, s, tm):
        return _ffn_half(x, ffn_norm[l, s], wg[l, s], wu[l, s], wd[l, s], tm=tm, tf=512)

    x_small = jnp.concatenate([x_sample.reshape(n_s, d), meta_tokens], axis=0)
    x_small = ffn(x_small, 0, 0, n_small)
    proj_s = _norm_matmul(x_small, rg_norm[0], w_in, tm=n_small, tn=1024)
    conv0 = jnp.concatenate([state_conv[0], jnp.zeros((1, CONV_W - 1, d_rnn), F32)], axis=0)
    h0 = jnp.concatenate([state_h[0], jnp.zeros((1, d_rnn), F32)], axis=0).reshape(bs + 1, 1, d_rnn)
    z_s, nconv_s, nh_s = _rg_core(proj_s.reshape(bs + 1, ts, 2 * d_rnn), conv0, h0, rg_conv_w[0], rg_conv_b[0],
                                  gate_w, rg_gate_b[0], rg_lambda[0], tt=ts, tc=512)
    x_small = _matmul_res(z_s.reshape(n_small, d_rnn), w_out, x_small, tm=n_small, tn=1024)
    x_small = ffn(x_small, 0, 1, n_small)
    k_s, v_s, khm_s, vhm_s, k0_s, k1_s, va_s = _kv_proj(x_small, kv_norm, w_kv_b, k_norm, tm=n_small,
                                                        packed=True, split=True, pos_mod=tq_p)

    xp = x_prompt.reshape(tp, d)
    xp = ffn(xp, 0, 0, tm_p)
    proj_p = _norm_matmul(xp, rg_norm[0], w_in, tm=tm_p, tn=1024)
    z_p, nconv_p, nh_p = _rg_core(proj_p.reshape(1, tp, 2 * d_rnn), nconv_s[bs:], nh_s[bs:], rg_conv_w[0],
                                  rg_conv_b[0], gate_w, rg_gate_b[0], rg_lambda[0], tt=tt_p, tc=512)
    xp = _matmul_res(z_p.reshape(tp, d_rnn), w_out, xp, tm=tm_p, tn=1024)
    xp = ffn(xp, 0, 1, tm_p)
    k_p, v_p, k0_p, k1_p, va_p = _kv_proj(xp, kv_norm, w_kv_b, k_norm, tm=tm_p, packed=False, split=True,
                                          pos_mod=tq_p)

    xp = ffn(xp, 1, 0, tm_p)
    qa_p, qb_p = _q_proj(xp, attn_norm[0], w_q_b, q_norm[0], slope_lanes, tm=tm_p, split=True)
    o_p = _attn_prompt(qa_p, qb_p, k0_p, k1_p, va_p, k0_s[:, n_s:], k1_s[:, n_s:], va_s[:, n_s:], slopes,
                       diff_lambda[0], sub_norm[0], tq=tq_p, lam_init=lam_init)
    xp = _matmul_res(o_p, w_o_b, xp, tm=tm_p, tn=1024)
    xp = ffn(xp, 1, 1, tm_p)

    xs = x_small[:n_s]
    xs = ffn(xs, 1, 0, n_s)
    (q_s,) = _q_proj(xs, attn_norm[0], w_q_b, q_norm[0], slope_lanes, tm=n_s, split=False)
    o_s = _attn_sample(q_s, cache_k.reshape(bs, past, -1), cache_v.reshape(bs, past, -1), khm_s, vhm_s,
                       slopes, diff_lambda[0], sub_norm[0], bs=bs, ts=ts, lam_init=lam_init)
    xs = _matmul_res(o_s, w_o_b, xs, tm=n_s, tn=1024)
    xs = ffn(xs, 1, 1, n_s)

    k_all = jnp.concatenate([k_s[n_s:], k_p], axis=0).reshape(1, N_META + tp, N_HEADS, 2, HEAD_DIM)
    v_all = jnp.concatenate([v_s[n_s:], v_p], axis=0).reshape(1, N_META + tp, N_HEADS, V_HEAD_DIM)
    return (
        xp.reshape(1, tp, d),
        xs.reshape(bs, ts, d),
        k_all,
        v_all,
        nconv_p.reshape(1, 1, CONV_W - 1, d_rnn),
        nh_p.reshape(1, 1, d_rnn),
        k_s[:n_s].reshape(bs, ts, N_HEADS, 2, HEAD_DIM),
        v_s[:n_s].reshape(bs, ts, N_HEADS, V_HEAD_DIM),
        nconv_s[:bs].reshape(1, bs, CONV_W - 1, d_rnn),
        nh_s[:bs].reshape(1, bs, d_rnn),
    )
```
